```python
import math
import jax, jax.numpy as jnp
from jax import lax
import numpy as np

D_MODEL = 1024
BATCH = 4
SEQ = 8192
DEPTH = 1

EPS = 1e-6
NEG_INF = -1e30

ATTN_HEADS = 8
ATTN_KV_HEADS = 2
ATTN_HEAD_DIM = 64
ATTN_WIDTH = ATTN_HEADS * ATTN_HEAD_DIM
ATTN_KV_WIDTH = ATTN_KV_HEADS * ATTN_HEAD_DIM
WINDOW = 128
BLOCK = 128

NUM_BUCKETS = 32
MAX_DISTANCE = 128

GLA_HEADS = 4
GLA_KEY_DIM = 64
GLA_VALUE_DIM = 128
GLA_KEY_WIDTH = GLA_HEADS * GLA_KEY_DIM
GLA_WIDTH = GLA_HEADS * GLA_VALUE_DIM
GLA_GATE_RANK = 16
GLA_GATE_NORMALIZER = 16.0
GLA_CHUNK = 64

MIX_WIDTH = ATTN_WIDTH + GLA_WIDTH
SPLIT_SIZES = (ATTN_WIDTH, ATTN_KV_WIDTH, ATTN_KV_WIDTH, ATTN_WIDTH,
               GLA_KEY_WIDTH, GLA_KEY_WIDTH, GLA_WIDTH, GLA_WIDTH, GLA_GATE_RANK)
N_IN = sum(SPLIT_SIZES)

kernel_name = "hybrid_swa_sink_gla_parallel_heads"


def rmsnorm(x, gain):
    xf = x.astype(jnp.float32)
    y = xf * lax.rsqrt(jnp.mean(xf * xf, axis=-1, keepdims=True) + EPS) * gain.astype(jnp.float32)
    return y.astype(x.dtype)


def t5_causal_bucket(dist):
    n = jnp.maximum(dist, 0)
    max_exact = NUM_BUCKETS // 2
    nf = jnp.maximum(n, 1).astype(jnp.float32)
    large = max_exact + (jnp.log(nf / max_exact) / math.log(MAX_DISTANCE / max_exact)
                         * (NUM_BUCKETS - max_exact)).astype(jnp.int32)
    large = jnp.minimum(large, NUM_BUCKETS - 1)
    return jnp.where(n < max_exact, n, large)


def sliding_window_attention(q, k, v, sinks, rel_bias):
    B, S = q.shape[0], q.shape[1]
    nb = S // BLOCK
    G = ATTN_HEADS // ATTN_KV_HEADS
    qb = q.reshape(B, nb, BLOCK, ATTN_KV_HEADS, G, ATTN_HEAD_DIM)
    kb = k.reshape(B, nb, BLOCK, ATTN_KV_HEADS, ATTN_HEAD_DIM)
    vb = v.reshape(B, nb, BLOCK, ATTN_KV_HEADS, ATTN_HEAD_DIM)

    def with_prev(t):
        prev = jnp.concatenate([jnp.zeros_like(t[:, :1]), t[:, :-1]], axis=1)
        return jnp.concatenate([prev, t], axis=2)

    kw, vw = with_prev(kb), with_prev(vb)
    scale = ATTN_HEAD_DIM ** -0.5
    scores = jnp.einsum("bnqkgd,bnckd->bnkgqc", qb, kw).astype(jnp.float32) * scale

    qi = jnp.arange(BLOCK)
    ci = jnp.arange(2 * BLOCK)
    dist = qi[:, None] + BLOCK - ci[None, :]
    in_window = (dist >= 0) & (dist < WINDOW)
    bias = rel_bias.astype(jnp.float32)[t5_causal_bucket(dist)]
    bias = bias.transpose(2, 0, 1).reshape(ATTN_KV_HEADS, G, BLOCK, 2 * BLOCK)
    key_exists = (jnp.arange(nb)[:, None] > 0) | (ci[None, :] >= BLOCK)
    mask = in_window[None, :, :] & key_exists[:, None, :]

    scores = jnp.where(mask[None, :, None, None], scores + bias[None, None], NEG_INF)
    sink = sinks.astype(jnp.float32).reshape(ATTN_KV_HEADS, G)[None, None, :, :, None, None]
    m = jnp.maximum(jnp.max(scores, axis=-1, keepdims=True), sink)
    p = jnp.exp(scores - m)
    probs = p / (jnp.sum(p, axis=-1, keepdims=True) + jnp.exp(sink - m))
    out = jnp.einsum("bnkgqc,bnckd->bnqkgd", probs.astype(v.dtype), vw)
    return out.reshape(B, S, ATTN_WIDTH)


def gla_chunked(q, k, v, g):
    B, S = q.shape[0], q.shape[1]
    nc = S // GLA_CHUNK
    f32 = jnp.float32

    def rs(t):
        return t.astype(f32).reshape(B, nc, GLA_CHUNK, GLA_HEADS, t.shape[-1])

    qc = rs(q) * (GLA_KEY_DIM ** -0.5)
    kc, vc, gc = rs(k), rs(v), rs(g)
    b = jnp.cumsum(gc, axis=2)
    b_last = b[:, :, -1]
    q_dec = qc * jnp.exp(b)
    k_dec = kc * jnp.exp(-b)
    k_to_end = kc * jnp.exp(b_last[:, :, None] - b)

    causal = jnp.tril(jnp.ones((GLA_CHUNK, GLA_CHUNK), dtype=bool))
    A = jnp.where(causal, jnp.einsum("bnihd,bnjhd->bnhij", q_dec, k_dec), 0.0)
    o_intra = jnp.einsum("bnhij,bnjhv->bnihv", A, vc)
    kv_upd = jnp.einsum("bnjhd,bnjhv->bnhdv", k_to_end, vc)

    def step(state, inp):
        q_d, decay, upd = inp
        o = jnp.einsum("bihd,bhdv->bihv", q_d, state)
        return decay[..., None] * state + upd, o

    init = jnp.zeros((B, GLA_HEADS, GLA_KEY_DIM, GLA_VALUE_DIM), f32)
    xs = (jnp.moveaxis(q_dec, 1, 0), jnp.moveaxis(jnp.exp(b_last), 1, 0), jnp.moveaxis(kv_upd, 1, 0))
    _, o_inter = lax.scan(step, init, xs)
    o = o_intra + jnp.moveaxis(o_inter, 0, 1)
    return o.reshape(B, S, GLA_HEADS, GLA_VALUE_DIM).astype(q.dtype)


def hybrid_layer(x, norm_gain, w_in, w_gate_up, b_gate, rel_bias, sinks, gla_norm_gain, w_out):
    B, S = x.shape[0], x.shape[1]
    h = rmsnorm(x, norm_gain)
    proj = h @ w_in
    idx = list(np.cumsum(SPLIT_SIZES)[:-1])
    q_a, k_a, v_a, z_a, q_g, k_g, v_g, z_g, r_g = jnp.split(proj, idx, axis=-1)

    attn = sliding_window_attention(
        q_a.reshape(B, S, ATTN_HEADS, ATTN_HEAD_DIM),
        k_a.reshape(B, S, ATTN_KV_HEADS, ATTN_HEAD_DIM),
        v_a.reshape(B, S, ATTN_KV_HEADS, ATTN_HEAD_DIM),
        sinks, rel_bias)
    attn = attn * jax.nn.silu(z_a)

    gk = jax.nn.log_sigmoid((r_g @ w_gate_up + b_gate).astype(jnp.float32)) / GLA_GATE_NORMALIZER
    o = gla_chunked(
        q_g.reshape(B, S, GLA_HEADS, GLA_KEY_DIM),
        k_g.reshape(B, S, GLA_HEADS, GLA_KEY_DIM),
        v_g.reshape(B, S, GLA_HEADS, GLA_VALUE_DIM),
        gk.reshape(B, S, GLA_HEADS, GLA_KEY_DIM))
    gla = rmsnorm(o, gla_norm_gain).reshape(B, S, GLA_WIDTH) * jax.nn.silu(z_g)

    y = jnp.concatenate([attn, gla], axis=-1) @ w_out
    return x + y


def setup_inputs(seed: int = 0) -> dict:
    key = jax.random.key(seed)
    ks = jax.random.split(key, 11)
    nrm = jax.random.normal
    return {
        "x": nrm(ks[0], (BATCH, SEQ, D_MODEL), jnp.float32),
        "norm_gain": 1.0 + 0.01 * nrm(ks[1], (DEPTH, D_MODEL), jnp.float32),
        "w_in": nrm(ks[2], (DEPTH, D_MODEL, N_IN), jnp.float32) * D_MODEL ** -0.5,
        "w_gate_up": nrm(ks[3], (DEPTH, GLA_GATE_RANK, GLA_KEY_WIDTH), jnp.float32) * GLA_GATE_RANK ** -0.5,
        "b_gate": 0.1 * nrm(ks[4], (DEPTH, GLA_KEY_WIDTH), jnp.float32),
        "rel_bias": 0.1 * nrm(ks[5], (NUM_BUCKETS, ATTN_HEADS), jnp.float32),
        "sinks": 0.5 * nrm(ks[6], (DEPTH, ATTN_HEADS), jnp.float32),
        "gla_norm_gain": 1.0 + 0.01 * nrm(ks[7], (DEPTH, GLA_VALUE_DIM), jnp.float32),
        "w_out": nrm(ks[8], (DEPTH, MIX_WIDTH, D_MODEL), jnp.float32) * MIX_WIDTH ** -0.5,
        "final_norm_gain": 1.0 + 0.01 * nrm(ks[9], (D_MODEL,), jnp.float32),
    }


def reference(x, norm_gain, w_in, w_gate_up, b_gate, rel_bias, sinks, gla_norm_gain, w_out, final_norm_gain):
    for l in range(DEPTH):
        x = hybrid_layer(x, norm_gain[l], w_in[l], w_gate_up[l], b_gate[l], rel_bias,
                         sinks[l], gla_norm_gain[l], w_out[l])
    return rmsnorm(x, final_norm_gain)
```

```python
import functools
import math

import jax
import jax.numpy as jnp
from jax import lax
from jax.experimental import pallas as pl
from jax.experimental.pallas import tpu as pltpu

F32 = jnp.float32
BF16 = jnp.bfloat16

D_MODEL = 1024
EPS = 1e-6
NEG_INF = -1e30

ATTN_HEADS = 8
ATTN_KV_HEADS = 2
HEAD_DIM = 64
BLOCK = 128
NUM_BUCKETS = 32
MAX_DISTANCE = 128

GLA_HEADS = 4
GLA_KEY_DIM = 64
GLA_VALUE_DIM = 128
GLA_GATE_RANK = 16
GLA_GATE_NORMALIZER = 16.0
GLA_CHUNK = 64

LANES = 128

OFF_QA = 0
OFF_KA = 512
OFF_VA = 640
OFF_ZA = 768
OFF_QG = 1280
OFF_KG = 1536
OFF_VG = 1792
OFF_ZG = 2304
OFF_RG = 2816
N_IN = 2832
N_PAD = OFF_RG + LANES
MIX_WIDTH = 1024
OFF_MIX_GLA = 512

TS = 256
PROJ_COLS = 256
VMEM_LIMIT_BYTES = 48 * 1024 * 1024

_NT = (((1,), (1,)), ((), ()))


def _sigmoid(z):
    return 1.0 / (1.0 + jnp.exp(-z))


def _split_halves(t, lo):
    zero = jnp.zeros_like(t)
    return jnp.concatenate([jnp.where(lo, t, zero), jnp.where(lo, zero, t)], axis=0)


def _fused_kernel(x_ref, ng_ref, win_ref, wg_ref, bg_ref, rb_ref, sk_ref, gg_ref, wout_ref, fg_ref,
                  out_ref,
                  proj, hb, mix, kbuf, vbuf, bias, state, qd, kd, ke):
    t = pl.program_id(1)
    first_call = jnp.logical_and(pl.program_id(0) == 0, t == 0)
    seq_start = t == 0

    @pl.when(first_call)
    def _build_bias():
        qi = lax.broadcasted_iota(jnp.int32, (BLOCK, 2 * BLOCK), 0)
        ci = lax.broadcasted_iota(jnp.int32, (BLOCK, 2 * BLOCK), 1)
        n = jnp.maximum(qi + BLOCK - ci, 0)
        max_exact = NUM_BUCKETS // 2
        nf = jnp.maximum(n, 1).astype(F32)
        large = max_exact + (jnp.log(nf / max_exact) / math.log(MAX_DISTANCE / max_exact)
                             * (NUM_BUCKETS - max_exact)).astype(jnp.int32)
        large = jnp.minimum(large, NUM_BUCKETS - 1)
        bucket = jnp.where(n < max_exact, n, large)

        def head_body(h, carry):
            def bucket_body(bk, acc):
                return jnp.where(bucket == bk, rb_ref[bk, h], acc)
            bias[h] = lax.fori_loop(0, NUM_BUCKETS, bucket_body,
                                    jnp.zeros((BLOCK, 2 * BLOCK), F32))
            return carry
        lax.fori_loop(0, ATTN_HEADS, head_body, 0)

    @pl.when(seq_start)
    def _reset():
        kbuf[:, 0:BLOCK, :] = jnp.zeros((4, BLOCK, LANES), BF16)
        vbuf[:, 0:BLOCK, :] = jnp.zeros((4, BLOCK, LANES), BF16)
        state[...] = jnp.zeros(state.shape, F32)

    x = x_ref[0]
    ms = jnp.mean(x * x, axis=-1, keepdims=True)
    hb[...] = (x * lax.rsqrt(ms + EPS) * ng_ref[...]).astype(BF16)
    for c0 in range(0, N_PAD, PROJ_COLS):
        c1 = min(c0 + PROJ_COLS, N_PAD)
        proj[:, c0:c1] = jnp.dot(hb[...], win_ref[:, c0:c1], preferred_element_type=F32)

    lane_t = lax.broadcasted_iota(jnp.int32, (TS, LANES), 1)
    lo_t = lane_t < HEAD_DIM
    zero_t = jnp.zeros((TS, LANES), F32)
    for src, buf in ((OFF_KA, kbuf), (OFF_VA, vbuf)):
        tile = proj[:, src:src + LANES]
        rolled = pltpu.roll(tile, HEAD_DIM, 1)
        buf[0, BLOCK:, :] = jnp.where(lo_t, tile, zero_t).astype(BF16)
        buf[1, BLOCK:, :] = jnp.where(lo_t, zero_t, rolled).astype(BF16)
        buf[2, BLOCK:, :] = jnp.where(lo_t, rolled, zero_t).astype(BF16)
        buf[3, BLOCK:, :] = jnp.where(lo_t, zero_t, tile).astype(BF16)

    qi = lax.broadcasted_iota(jnp.int32, (BLOCK, 2 * BLOCK), 0)
    ci = lax.broadcasted_iota(jnp.int32, (BLOCK, 2 * BLOCK), 1)
    dist = qi + BLOCK - ci
    in_window = jnp.logical_and(dist >= 0, dist < BLOCK)
    lane_o = lax.broadcasted_iota(jnp.int32, (2 * BLOCK, LANES), 1)
    lo_o = lane_o < HEAD_DIM
    scale = HEAD_DIM ** -0.5

    for j in range(TS // BLOCK):
        r0 = j * BLOCK
        if j == 0:
            key_exists = jnp.logical_or(ci >= BLOCK, jnp.logical_not(seq_start))
            mask = jnp.logical_and(in_window, key_exists)
        else:
            mask = in_window
        for kv in range(ATTN_KV_HEADS):
            q2 = proj[r0:r0 + BLOCK, OFF_QA + kv * 256:OFF_QA + (kv + 1) * 256] * scale
            lhs = jnp.concatenate([q2[:, :LANES], q2[:, LANES:]], axis=0).astype(BF16)
            probs = []
            rinv = []
            for par in range(2):
                kwin = kbuf[2 * kv + par, r0:r0 + 2 * BLOCK, :]
                s2 = lax.dot_general(lhs, kwin, _NT, preferred_element_type=F32)
                ps = []
                rs = []
                for i in range(2):
                    head = 4 * kv + 2 * i + par
                    s = s2[i * BLOCK:(i + 1) * BLOCK]
                    s = jnp.where(mask, s + bias[head], NEG_INF)
                    sink = sk_ref[head]
                    m = jnp.maximum(jnp.max(s, axis=-1, keepdims=True), sink)
                    p = jnp.exp(s - m)
                    den = jnp.sum(p, axis=-1, keepdims=True) + jnp.exp(sink - m)
                    ps.append(p.astype(BF16))
                    rs.append(1.0 / den)
                probs.append(jnp.concatenate(ps, axis=0))
                rinv.append(jnp.concatenate(rs, axis=0))
            o2 = (jnp.dot(probs[0], vbuf[2 * kv, r0:r0 + 2 * BLOCK, :], preferred_element_type=F32)
                  + jnp.dot(probs[1], vbuf[2 * kv + 1, r0:r0 + 2 * BLOCK, :],
                            preferred_element_type=F32))
            o2 = o2 * jnp.where(lo_o, rinv[0], rinv[1])
            for i in range(2):
                tile_idx = 2 * kv + i
                z = proj[r0:r0 + BLOCK, OFF_ZA + tile_idx * LANES:OFF_ZA + (tile_idx + 1) * LANES]
                gated = o2[i * BLOCK:(i + 1) * BLOCK] * (z * _sigmoid(z))
                mix[r0:r0 + BLOCK, tile_idx * LANES:(tile_idx + 1) * LANES] = gated.astype(BF16)

    kbuf[:, 0:BLOCK, :] = kbuf[:, TS:TS + BLOCK, :]
    vbuf[:, 0:BLOCK, :] = vbuf[:, TS:TS + BLOCK, :]

    r_lo = proj[:, OFF_RG:OFF_RG + LANES].astype(BF16)
    gpre = jnp.dot(r_lo, wg_ref[...], preferred_element_type=F32) + bg_ref[...]
    gk = (jnp.minimum(gpre, 0.0) - jnp.log1p(jnp.exp(-jnp.abs(gpre)))) * (1.0 / GLA_GATE_NORMALIZER)

    ri = lax.broadcasted_iota(jnp.int32, (GLA_CHUNK, LANES), 0)
    li = lax.broadcasted_iota(jnp.int32, (GLA_CHUNK, LANES), 1)
    lo_c = li < GLA_KEY_DIM
    causal = (li & (GLA_KEY_DIM - 1)) <= ri
    tri2 = jnp.where(causal, 1.0, 0.0).astype(BF16)
    qscale = GLA_KEY_DIM ** -0.5

    decay_rows = []
    n_chunks = TS // GLA_CHUNK
    for c in range(n_chunks):
        rows = slice(c * GLA_CHUNK, (c + 1) * GLA_CHUNK)
        g_c = gk[rows]
        g_hi = g_c.astype(BF16)
        g_lo = (g_c - g_hi.astype(F32)).astype(BF16)
        b_c = jnp.dot(tri2, jnp.concatenate([g_hi, g_lo], axis=0), preferred_element_type=F32)
        b_last = b_c[GLA_CHUNK - 1:GLA_CHUNK, :]
        qd[rows, :] = proj[rows, OFF_QG:OFF_QG + 256] * qscale * jnp.exp(b_c)
        kg = proj[rows, OFF_KG:OFF_KG + 256]
        kd[rows, :] = kg * jnp.exp(-b_c)
        ke[rows, :] = kg * jnp.exp(b_last - b_c)
        decay_rows.append(jnp.exp(b_last))

    gg = gg_ref[...]
    for c in range(n_chunks):
        rows = slice(c * GLA_CHUNK, (c + 1) * GLA_CHUNK)
        for p in range(GLA_HEADS // 2):
            cols = slice(p * LANES, (p + 1) * LANES)
            qt = qd[rows, cols]
            kst = _split_halves(kd[rows, cols], lo_c).astype(BF16)
            a = lax.dot_general(qt.astype(BF16), kst, _NT, preferred_element_type=F32)
            a = jnp.where(causal, a, 0.0)
            ast = _split_halves(a, lo_c).astype(BF16)
            vst = jnp.concatenate(
                [proj[rows, OFF_VG + (2 * p) * LANES:OFF_VG + (2 * p + 1) * LANES],
                 proj[rows, OFF_VG + (2 * p + 1) * LANES:OFF_VG + (2 * p + 2) * LANES]], axis=0)
            qst = _split_halves(qt, lo_c).astype(BF16)
            st = state[p]
            o = (jnp.dot(ast, vst.astype(BF16), preferred_element_type=F32)
                 + lax.dot_general(qst, st.astype(BF16), _NT, preferred_element_type=F32))
            kest = _split_halves(ke[rows, cols], lo_c).astype(BF16)
            ut = jnp.dot(vst.T.astype(BF16), kest, preferred_element_type=F32)
            state[p] = st * decay_rows[c][:, cols] + ut

            oms = jnp.mean(o * o, axis=-1, keepdims=True)
            on = o * lax.rsqrt(oms + EPS) * gg
            for e in range(2):
                head = 2 * p + e
                z = proj[rows, OFF_ZG + head * LANES:OFF_ZG + (head + 1) * LANES]
                gated = on[e * GLA_CHUNK:(e + 1) * GLA_CHUNK] * (z * _sigmoid(z))
                mix[rows, OFF_MIX_GLA + head * LANES:OFF_MIX_GLA + (head + 1) * LANES] = (
                    gated.astype(BF16))

    for c0 in range(0, D_MODEL, PROJ_COLS):
        y = jnp.dot(mix[...], wout_ref[:, c0:c0 + PROJ_COLS], preferred_element_type=F32)
        out_ref[0, :, c0:c0 + PROJ_COLS] = x_ref[0, :, c0:c0 + PROJ_COLS] + y
    xr = out_ref[0]
    ms2 = jnp.mean(xr * xr, axis=-1, keepdims=True)
    out_ref[0] = xr * lax.rsqrt(ms2 + EPS) * fg_ref[...]


@jax.jit
def kernel(x, norm_gain, w_in, w_gate_up, b_gate, rel_bias, sinks, gla_norm_gain, w_out,
           final_norm_gain):
    batch, seq, d_model = x.shape
    assert d_model == D_MODEL and seq % TS == 0
    assert norm_gain.shape[0] == 1, "single-layer block"
    w_in_p = jnp.pad(w_in[0], ((0, 0), (0, N_PAD - N_IN))).astype(BF16)
    wg_p = jnp.pad(w_gate_up[0], ((0, LANES - GLA_GATE_RANK), (0, 0))).astype(BF16)
    const = lambda *shape: pl.BlockSpec(shape, lambda b, t: (0,) * len(shape))
    smem = pl.BlockSpec(memory_space=pltpu.SMEM)
    return pl.pallas_call(
        _fused_kernel,
        grid=(batch, seq // TS),
        in_specs=[
            pl.BlockSpec((1, TS, D_MODEL), lambda b, t: (b, t, 0)),
            const(1, D_MODEL),
            const(D_MODEL, N_PAD),
            const(LANES, 256),
            const(1, 256),
            smem,
            smem,
            const(1, GLA_VALUE_DIM),
            const(MIX_WIDTH, D_MODEL),
            const(1, D_MODEL),
        ],
        out_specs=pl.BlockSpec((1, TS, D_MODEL), lambda b, t: (b, t, 0)),
        out_shape=jax.ShapeDtypeStruct(x.shape, x.dtype),
        scratch_shapes=[
            pltpu.VMEM((TS, N_PAD), F32),
            pltpu.VMEM((TS, D_MODEL), BF16),
            pltpu.VMEM((TS, MIX_WIDTH), BF16),
            pltpu.VMEM((4, TS + BLOCK, LANES), BF16),
            pltpu.VMEM((4, TS + BLOCK, LANES), BF16),
            pltpu.VMEM((ATTN_HEADS, BLOCK, 2 * BLOCK), F32),
            pltpu.VMEM((GLA_HEADS // 2, GLA_VALUE_DIM, LANES), F32),
            pltpu.VMEM((TS, 256), F32),
            pltpu.VMEM((TS, 256), F32),
            pltpu.VMEM((TS, 256), F32),
        ],
        compiler_params=pltpu.CompilerParams(
            dimension_semantics=("arbitrary", "arbitrary"),
            vmem_limit_bytes=VMEM_LIMIT_BYTES),
        name="hybrid_swa_gla_block",
    )(x, norm_gain, w_in_p, wg_p, b_gate, rel_bias, sinks[0], gla_norm_gain, w_out[0].astype(BF16),
      final_norm_gain.reshape(1, D_MODEL))
```

```python
import functools
import math

import jax
import jax.numpy as jnp
from jax import lax
from jax.experimental import pallas as pl
from jax.experimental.pallas import tpu as pltpu

F32 = jnp.float32
BF16 = jnp.bfloat16

D_MODEL = 1024
EPS = 1e-6
NEG_INF = -1e30

ATTN_HEADS = 8
ATTN_KV_HEADS = 2
HEAD_DIM = 64
BLOCK = 128
NUM_BUCKETS = 32
MAX_DISTANCE = 128

GLA_HEADS = 4
GLA_KEY_DIM = 64
GLA_VALUE_DIM = 128
GLA_GATE_RANK = 16
GLA_GATE_NORMALIZER = 16.0
GLA_CHUNK = 64

LANES = 128

OFF_QA = 0
OFF_KA = 512
OFF_VA = 640
OFF_ZA = 768
OFF_QG = 1280
OFF_KG = 1536
OFF_VG = 1792
OFF_ZG = 2304
OFF_RG = 2816
N_IN = 2832
N_PAD = OFF_RG + LANES
MIX_WIDTH = 1024
OFF_MIX_GLA = 512

TS = 256
PROJ_COLS = 256
VMEM_LIMIT_BYTES = 48 * 1024 * 1024

_NT = (((1,), (1,)), ((), ()))


def _sigmoid(z):
    return 1.0 / (1.0 + jnp.exp(-z))


def _split_halves(t, lo):
    zero = jnp.zeros_like(t)
    return jnp.concatenate([jnp.where(lo, t, zero), jnp.where(lo, zero, t)], axis=0)


def _interleave(main, fill):
    order = []
    done = 0
    for i, piece in enumerate(main):
        order.append(piece)
        want = ((i + 1) * len(fill)) // len(main)
        order.extend(fill[done:want])
        done = want
    return order


def _fused_kernel(tiles_per_seq, xp_ref, xn_ref, ng_ref, win_ref, wg_ref, bg_ref, rb_ref, sk_ref,
                  gg_ref, wout_ref, fg_ref,
                  out_ref,
                  proj2, mix2, hb, ybuf, kbuf, vbuf, bias, state, gk_s, qd, kd, ke):
    g = pl.program_id(0)
    first_call = g == 0
    seq_start = lax.rem(g, tiles_per_seq) == 0

    lane_s = lax.broadcasted_iota(jnp.int32, (TS, LANES), 1)
    lo_s = lane_s < HEAD_DIM
    zero_s = jnp.zeros((TS, LANES), F32)
    qi = lax.broadcasted_iota(jnp.int32, (BLOCK, 2 * BLOCK), 0)
    ci = lax.broadcasted_iota(jnp.int32, (BLOCK, 2 * BLOCK), 1)
    dist = qi + BLOCK - ci
    in_window = jnp.logical_and(dist >= 0, dist < BLOCK)
    lane_o = lax.broadcasted_iota(jnp.int32, (2 * BLOCK, LANES), 1)
    lo_o = lane_o < HEAD_DIM
    scale = HEAD_DIM ** -0.5
    ri = lax.broadcasted_iota(jnp.int32, (GLA_CHUNK, LANES), 0)
    li = lax.broadcasted_iota(jnp.int32, (GLA_CHUNK, LANES), 1)
    lo_c = li < GLA_KEY_DIM
    causal = (li & (GLA_KEY_DIM - 1)) <= ri
    tri2 = jnp.where(causal, 1.0, 0.0).astype(BF16)
    qscale = GLA_KEY_DIM ** -0.5
    n_units = (TS // BLOCK) * ATTN_KV_HEADS
    n_chunks = TS // GLA_CHUNK

    def norm_in(src_ref):
        x = src_ref[...]
        ms = jnp.mean(x * x, axis=-1, keepdims=True)
        hb[...] = (x * lax.rsqrt(ms + EPS) * ng_ref[...]).astype(BF16)

    def in_chunk(proj, c0):
        c1 = min(c0 + PROJ_COLS, N_PAD)
        proj[:, c0:c1] = jnp.dot(hb[...], win_ref[:, c0:c1], preferred_element_type=F32)

    def out_chunk(mix, c0):
        ybuf[:, c0:c0 + PROJ_COLS] = jnp.dot(mix[...], wout_ref[:, c0:c0 + PROJ_COLS],
                                             preferred_element_type=F32)

    def finalize():
        xr = xp_ref[...] + ybuf[...]
        ms2 = jnp.mean(xr * xr, axis=-1, keepdims=True)
        out_ref[...] = xr * lax.rsqrt(ms2 + EPS) * fg_ref[...]

    def step(cur, nxt):
        proj = proj2.at[cur]
        mix = mix2.at[cur]
        vals = {}

        def window_build():
            for src, buf in ((OFF_KA, kbuf), (OFF_VA, vbuf)):
                tile = proj[:, src:src + LANES]
                rolled = pltpu.roll(tile, HEAD_DIM, 1)
                buf[0, BLOCK:, :] = jnp.where(lo_s, tile, zero_s).astype(BF16)
                buf[1, BLOCK:, :] = jnp.where(lo_s, zero_s, rolled).astype(BF16)
                buf[2, BLOCK:, :] = jnp.where(lo_s, rolled, zero_s).astype(BF16)
                buf[3, BLOCK:, :] = jnp.where(lo_s, zero_s, tile).astype(BF16)

        def swa_qk(u):
            j, kv = divmod(u, ATTN_KV_HEADS)
            r0 = j * BLOCK
            q2 = proj[r0:r0 + BLOCK, OFF_QA + kv * 256:OFF_QA + (kv + 1) * 256] * scale
            lhs = jnp.concatenate([q2[:, :LANES], q2[:, LANES:]], axis=0).astype(BF16)
            vals['s', u] = [
                lax.dot_general(lhs, kbuf[2 * kv + par, r0:r0 + 2 * BLOCK, :], _NT,
                                preferred_element_type=F32)
                for par in range(2)]

        def swa_pv(u):
            j, kv = divmod(u, ATTN_KV_HEADS)
            r0 = j * BLOCK
            if r0 == 0:
                key_exists = jnp.logical_or(ci >= BLOCK, jnp.logical_not(seq_start))
                mask = jnp.logical_and(in_window, key_exists)
            else:
                mask = in_window
            probs = []
            rinv = []
            s_pair = vals.pop(('s', u))
            for par in range(2):
                s2 = s_pair[par]
                ps = []
                rs = []
                for i in range(2):
                    head = 4 * kv + 2 * i + par
                    s = s2[i * BLOCK:(i + 1) * BLOCK]
                    s = jnp.where(mask, s + bias[head], NEG_INF)
                    sink = sk_ref[head]
                    m = jnp.maximum(jnp.max(s, axis=-1, keepdims=True), sink)
                    p = jnp.exp(s - m)
                    den = jnp.sum(p, axis=-1, keepdims=True) + jnp.exp(sink - m)
                    ps.append(p.astype(BF16))
                    rs.append(1.0 / den)
                probs.append(jnp.concatenate(ps, axis=0))
                rinv.append(jnp.concatenate(rs, axis=0))
            o2 = (jnp.dot(probs[0], vbuf[2 * kv, r0:r0 + 2 * BLOCK, :], preferred_element_type=F32)
                  + jnp.dot(probs[1], vbuf[2 * kv + 1, r0:r0 + 2 * BLOCK, :],
                            preferred_element_type=F32))
            o2 = o2 * jnp.where(lo_o, rinv[0], rinv[1])
            for i in range(2):
                tile_idx = 2 * kv + i
                z = proj[r0:r0 + BLOCK, OFF_ZA + tile_idx * LANES:OFF_ZA + (tile_idx + 1) * LANES]
                gated = o2[i * BLOCK:(i + 1) * BLOCK] * (z * _sigmoid(z))
                mix[r0:r0 + BLOCK, tile_idx * LANES:(tile_idx + 1) * LANES] = gated.astype(BF16)

        def window_carry():
            kbuf[:, 0:BLOCK, :] = kbuf[:, TS:TS + BLOCK, :]
            vbuf[:, 0:BLOCK, :] = vbuf[:, TS:TS + BLOCK, :]

        def gla_gate():
            r_lo = proj[:, OFF_RG:OFF_RG + LANES].astype(BF16)
            gpre = jnp.dot(r_lo, wg_ref[...], preferred_element_type=F32) + bg_ref[...]
            gk_s[...] = ((jnp.minimum(gpre, 0.0) - jnp.log1p(jnp.exp(-jnp.abs(gpre))))
                         * (1.0 / GLA_GATE_NORMALIZER))

        def gla_decay(c):
            rows = slice(c * GLA_CHUNK, (c + 1) * GLA_CHUNK)
            g_c = gk_s[rows, :]
            g_hi = g_c.astype(BF16)
            g_lo = (g_c - g_hi.astype(F32)).astype(BF16)
            b_c = jnp.dot(tri2, jnp.concatenate([g_hi, g_lo], axis=0), preferred_element_type=F32)
            b_last = b_c[GLA_CHUNK - 1:GLA_CHUNK, :]
            qd[rows, :] = proj[rows, OFF_QG:OFF_QG + 256] * qscale * jnp.exp(b_c)
            kg = proj[rows, OFF_KG:OFF_KG + 256]
            kd[rows, :] = kg * jnp.exp(-b_c)
            ke[rows, :] = kg * jnp.exp(b_last - b_c)
            vals['decay', c] = jnp.exp(b_last)

        def gla_intra(c):
            rows = slice(c * GLA_CHUNK, (c + 1) * GLA_CHUNK)
            for p in range(GLA_HEADS // 2):
                cols = slice(p * LANES, (p + 1) * LANES)
                qt = qd[rows, cols]
                kst = _split_halves(kd[rows, cols], lo_c).astype(BF16)
                a = lax.dot_general(qt.astype(BF16), kst, _NT, preferred_element_type=F32)
                a = jnp.where(causal, a, 0.0)
                vals['ast', c, p] = _split_halves(a, lo_c).astype(BF16)

        def gla_out(c):
            rows = slice(c * GLA_CHUNK, (c + 1) * GLA_CHUNK)
            gg = gg_ref[...]
            for p in range(GLA_HEADS // 2):
                cols = slice(p * LANES, (p + 1) * LANES)
                vst = jnp.concatenate(
                    [proj[rows, OFF_VG + (2 * p) * LANES:OFF_VG + (2 * p + 1) * LANES],
                     proj[rows, OFF_VG + (2 * p + 1) * LANES:OFF_VG + (2 * p + 2) * LANES]], axis=0)
                qst = _split_halves(qd[rows, cols], lo_c).astype(BF16)
                st = state[p]
                o = (jnp.dot(vals.pop(('ast', c, p)), vst.astype(BF16), preferred_element_type=F32)
                     + lax.dot_general(qst, st.astype(BF16), _NT, preferred_element_type=F32))
                kest = _split_halves(ke[rows, cols], lo_c).astype(BF16)
                ut = jnp.dot(vst.T.astype(BF16), kest, preferred_element_type=F32)
                state[p] = st * vals['decay', c][:, cols] + ut

                oms = jnp.mean(o * o, axis=-1, keepdims=True)
                on = o * lax.rsqrt(oms + EPS) * gg
                for e in range(2):
                    head = 2 * p + e
                    z = proj[rows, OFF_ZG + head * LANES:OFF_ZG + (head + 1) * LANES]
                    gated = on[e * GLA_CHUNK:(e + 1) * GLA_CHUNK] * (z * _sigmoid(z))
                    mix[rows, OFF_MIX_GLA + head * LANES:OFF_MIX_GLA + (head + 1) * LANES] = (
                        gated.astype(BF16))

        P = functools.partial
        main = [gla_gate, window_build, P(swa_qk, 0), P(swa_qk, 1)]
        main += [P(gla_decay, c) for c in range(n_chunks)]
        for u in range(n_units):
            main.append(P(swa_pv, u))
            if u + 2 < n_units:
                main.append(P(swa_qk, u + 2))
        main += [window_carry, P(gla_intra, 0), P(gla_intra, 1)]
        for c in range(n_chunks):
            main.append(P(gla_out, c))
            if c + 2 < n_chunks:
                main.append(P(gla_intra, c + 2))
        fill = [P(out_chunk, mix2.at[nxt], c0) for c0 in range(0, D_MODEL, PROJ_COLS)]
        fill += [P(norm_in, xn_ref), finalize]
        fill += [P(in_chunk, proj2.at[nxt], c0) for c0 in range(0, N_PAD, PROJ_COLS)]
        for piece in _interleave(main, fill):
            piece()

    @pl.when(first_call)
    def _prologue():
        qb = lax.broadcasted_iota(jnp.int32, (BLOCK, 2 * BLOCK), 0)
        cb = lax.broadcasted_iota(jnp.int32, (BLOCK, 2 * BLOCK), 1)
        n = jnp.maximum(qb + BLOCK - cb, 0)
        max_exact = NUM_BUCKETS // 2
        nf = jnp.maximum(n, 1).astype(F32)
        large = max_exact + (jnp.log(nf / max_exact) / math.log(MAX_DISTANCE / max_exact)
                             * (NUM_BUCKETS - max_exact)).astype(jnp.int32)
        large = jnp.minimum(large, NUM_BUCKETS - 1)
        bucket = jnp.where(n < max_exact, n, large)

        def head_body(h, carry):
            def bucket_body(bk, acc):
                return jnp.where(bucket == bk, rb_ref[bk, h], acc)
            bias[h] = lax.fori_loop(0, NUM_BUCKETS, bucket_body,
                                    jnp.zeros((BLOCK, 2 * BLOCK), F32))
            return carry
        lax.fori_loop(0, ATTN_HEADS, head_body, 0)
        norm_in(xp_ref)
        for c0 in range(0, N_PAD, PROJ_COLS):
            in_chunk(proj2.at[0], c0)
        mix2[1] = jnp.zeros((TS, MIX_WIDTH), BF16)

    @pl.when(seq_start)
    def _reset():
        kbuf[:, 0:BLOCK, :] = jnp.zeros((4, BLOCK, LANES), BF16)
        vbuf[:, 0:BLOCK, :] = jnp.zeros((4, BLOCK, LANES), BF16)
        state[...] = jnp.zeros(state.shape, F32)

    parity = lax.rem(g, 2)

    @pl.when(parity == 0)
    def _even():
        step(0, 1)

    @pl.when(parity == 1)
    def _odd():
        step(1, 0)


@jax.jit
def kernel(x, norm_gain, w_in, w_gate_up, b_gate, rel_bias, sinks, gla_norm_gain, w_out,
           final_norm_gain):
    batch, seq, d_model = x.shape
    assert d_model == D_MODEL and seq % TS == 0
    assert norm_gain.shape[0] == 1, "single-layer block"
    tiles_per_seq = seq // TS
    n_tiles = batch * tiles_per_seq
    w_in_p = jnp.pad(w_in[0], ((0, 0), (0, N_PAD - N_IN))).astype(BF16)
    wg_p = jnp.pad(w_gate_up[0], ((0, LANES - GLA_GATE_RANK), (0, 0))).astype(BF16)
    x2 = x.reshape(batch * seq, d_model)
    const = lambda *shape: pl.BlockSpec(shape, lambda g: (0,) * len(shape))
    smem = pl.BlockSpec(memory_space=pltpu.SMEM)
    prev_tile = lambda g: (jnp.maximum(g - 1, 0), 0)
    next_tile = lambda g: (jnp.minimum(g + 1, n_tiles - 1), 0)
    out = pl.pallas_call(
        functools.partial(_fused_kernel, tiles_per_seq),
        grid=(n_tiles + 1,),
        in_specs=[
            pl.BlockSpec((TS, D_MODEL), prev_tile),
            pl.BlockSpec((TS, D_MODEL), next_tile),
            const(1, D_MODEL),
            const(D_MODEL, N_PAD),
            const(LANES, 256),
            const(1, 256),
            smem,
            smem,
            const(1, GLA_VALUE_DIM),
            const(MIX_WIDTH, D_MODEL),
            const(1, D_MODEL),
        ],
        out_specs=pl.BlockSpec((TS, D_MODEL), prev_tile),
        out_shape=jax.ShapeDtypeStruct(x2.shape, x.dtype),
        scratch_shapes=[
            pltpu.VMEM((2, TS, N_PAD), F32),
            pltpu.VMEM((2, TS, MIX_WIDTH), BF16),
            pltpu.VMEM((TS, D_MODEL), BF16),
            pltpu.VMEM((TS, D_MODEL), F32),
            pltpu.VMEM((4, TS + BLOCK, LANES), BF16),
            pltpu.VMEM((4, TS + BLOCK, LANES), BF16),
            pltpu.VMEM((ATTN_HEADS, BLOCK, 2 * BLOCK), F32),
            pltpu.VMEM((GLA_HEADS // 2, GLA_VALUE_DIM, LANES), F32),
            pltpu.VMEM((TS, 256), F32),
            pltpu.VMEM((TS, 256), F32),
            pltpu.VMEM((TS, 256), F32),
            pltpu.VMEM((TS, 256), F32),
        ],
        compiler_params=pltpu.CompilerParams(
            dimension_semantics=("arbitrary",),
            vmem_limit_bytes=VMEM_LIMIT_BYTES),
        name="hybrid_swa_gla_block",
    )(x2, x2, norm_gain, w_in_p, wg_p, b_gate, rel_bias, sinks[0], gla_norm_gain,
      w_out[0].astype(BF16), final_norm_gain.reshape(1, D_MODEL))
    return out.reshape(batch, seq, d_model)
```

```python
import functools
import math

import jax
import jax.numpy as jnp
from jax import lax
from jax.experimental import pallas as pl
from jax.experimental.pallas import tpu as pltpu

F32 = jnp.float32
BF16 = jnp.bfloat16

D_MODEL = 1024
EPS = 1e-6
NEG_INF = -1e30

ATTN_HEADS = 8
ATTN_KV_HEADS = 2
HEAD_DIM = 64
BLOCK = 128
NUM_BUCKETS = 32
MAX_DISTANCE = 128

GLA_HEADS = 4
GLA_KEY_DIM = 64
GLA_VALUE_DIM = 128
GLA_GATE_RANK = 16
GLA_GATE_NORMALIZER = 16.0
GLA_CHUNK = 64

LANES = 128

OFF_QA = 0
OFF_KA = 512
OFF_VA = 640
OFF_ZA = 768
OFF_QG = 1280
OFF_KG = 1536
OFF_VG = 1792
OFF_ZG = 2304
OFF_RG = 2816
N_IN = 2832
N_PAD = OFF_RG + LANES
MIX_WIDTH = 1024
OFF_MIX_GLA = 512

TS = 256
SUB = 256
PROJ_COLS = 256
VMEM_LIMIT_BYTES = 52 * 1024 * 1024

_NT = (((1,), (1,)), ((), ()))


def _sigmoid(z):
    return 1.0 / (1.0 + jnp.exp(-z))


def _split_halves(t, lo):
    zero = jnp.zeros_like(t)
    return jnp.concatenate([jnp.where(lo, t, zero), jnp.where(lo, zero, t)], axis=0)


def _interleave(main, fill):
    order = []
    done = 0
    for i, piece in enumerate(main):
        order.append(piece)
        want = ((i + 1) * len(fill)) // len(main)
        order.extend(fill[done:want])
        done = want
    return order


def _fused_kernel(tiles_per_seq, xp_ref, xn_ref, ng_ref, win_ref, wg_ref, bg_ref, rb_ref, sk_ref,
                  gg_ref, wout_ref, fg_ref,
                  out_ref,
                  proj2, mix2, hb, ybuf, kbuf, vbuf, bias, state, gk_s, qd, kd, ke):
    g = pl.program_id(0)
    first_call = g == 0
    seq_start = lax.rem(g, tiles_per_seq) == 0

    lane_s = lax.broadcasted_iota(jnp.int32, (TS, LANES), 1)
    lo_s = lane_s < HEAD_DIM
    zero_s = jnp.zeros((TS, LANES), F32)
    qi = lax.broadcasted_iota(jnp.int32, (BLOCK, 2 * BLOCK), 0)
    ci = lax.broadcasted_iota(jnp.int32, (BLOCK, 2 * BLOCK), 1)
    dist = qi + BLOCK - ci
    in_window = jnp.logical_and(dist >= 0, dist < BLOCK)
    lane_o = lax.broadcasted_iota(jnp.int32, (2 * BLOCK, LANES), 1)
    lo_o = lane_o < HEAD_DIM
    scale = HEAD_DIM ** -0.5
    ri = lax.broadcasted_iota(jnp.int32, (GLA_CHUNK, LANES), 0)
    li = lax.broadcasted_iota(jnp.int32, (GLA_CHUNK, LANES), 1)
    lo_c = li < GLA_KEY_DIM
    causal = (li & (GLA_KEY_DIM - 1)) <= ri
    tri2 = jnp.where(causal, 1.0, 0.0).astype(BF16)
    qscale = GLA_KEY_DIM ** -0.5
    n_units = (TS // BLOCK) * ATTN_KV_HEADS
    n_chunks = TS // GLA_CHUNK

    def norm_in(src_ref):
        x = src_ref[...]
        ms = jnp.mean(x * x, axis=-1, keepdims=True)
        hb[...] = (x * lax.rsqrt(ms + EPS) * ng_ref[...]).astype(BF16)

    def in_chunk(proj, r0, c0):
        c1 = min(c0 + PROJ_COLS, N_PAD)
        proj[r0:r0 + SUB, c0:c1] = jnp.dot(hb[r0:r0 + SUB, :], win_ref[:, c0:c1],
                                           preferred_element_type=F32)

    def out_chunk(mix, r0, c0):
        ybuf[r0:r0 + SUB, c0:c0 + PROJ_COLS] = jnp.dot(
            mix[r0:r0 + SUB, :], wout_ref[:, c0:c0 + PROJ_COLS], preferred_element_type=F32)

    def finalize():
        xr = xp_ref[...] + ybuf[...]
        ms2 = jnp.mean(xr * xr, axis=-1, keepdims=True)
        out_ref[...] = xr * lax.rsqrt(ms2 + EPS) * fg_ref[...]

    def step(cur, nxt):
        proj = proj2.at[cur]
        mix = mix2.at[cur]
        vals = {}

        def window_build():
            for src, buf in ((OFF_KA, kbuf), (OFF_VA, vbuf)):
                tile = proj[:, src:src + LANES]
                rolled = pltpu.roll(tile, HEAD_DIM, 1)
                buf[0, BLOCK:, :] = jnp.where(lo_s, tile, zero_s).astype(BF16)
                buf[1, BLOCK:, :] = jnp.where(lo_s, zero_s, rolled).astype(BF16)
                buf[2, BLOCK:, :] = jnp.where(lo_s, rolled, zero_s).astype(BF16)
                buf[3, BLOCK:, :] = jnp.where(lo_s, zero_s, tile).astype(BF16)

        def swa_qk(u):
            j, kv = divmod(u, ATTN_KV_HEADS)
            r0 = j * BLOCK
            q2 = proj[r0:r0 + BLOCK, OFF_QA + kv * 256:OFF_QA + (kv + 1) * 256] * scale
            lhs = jnp.concatenate([q2[:, :LANES], q2[:, LANES:]], axis=0).astype(BF16)
            vals['s', u] = [
                lax.dot_general(lhs, kbuf[2 * kv + par, r0:r0 + 2 * BLOCK, :], _NT,
                                preferred_element_type=F32)
                for par in range(2)]

        def swa_pv(u):
            j, kv = divmod(u, ATTN_KV_HEADS)
            r0 = j * BLOCK
            if r0 == 0:
                key_exists = jnp.logical_or(ci >= BLOCK, jnp.logical_not(seq_start))
                mask = jnp.logical_and(in_window, key_exists)
            else:
                mask = in_window
            probs = []
            rinv = []
            s_pair = vals.pop(('s', u))
            for par in range(2):
                s2 = s_pair[par]
                ps = []
                rs = []
                for i in range(2):
                    head = 4 * kv + 2 * i + par
                    s = s2[i * BLOCK:(i + 1) * BLOCK]
                    s = jnp.where(mask, s + bias[head], NEG_INF)
                    sink = sk_ref[head]
                    m = jnp.maximum(jnp.max(s, axis=-1, keepdims=True), sink)
                    p = jnp.exp(s - m)
                    den = jnp.sum(p, axis=-1, keepdims=True) + jnp.exp(sink - m)
                    ps.append(p.astype(BF16))
                    rs.append(1.0 / den)
                probs.append(jnp.concatenate(ps, axis=0))
                rinv.append(jnp.concatenate(rs, axis=0))
            o2 = (jnp.dot(probs[0], vbuf[2 * kv, r0:r0 + 2 * BLOCK, :], preferred_element_type=F32)
                  + jnp.dot(probs[1], vbuf[2 * kv + 1, r0:r0 + 2 * BLOCK, :],
                            preferred_element_type=F32))
            o2 = o2 * jnp.where(lo_o, rinv[0], rinv[1])
            for i in range(2):
                tile_idx = 2 * kv + i
                z = proj[r0:r0 + BLOCK, OFF_ZA + tile_idx * LANES:OFF_ZA + (tile_idx + 1) * LANES]
                gated = o2[i * BLOCK:(i + 1) * BLOCK] * (z * _sigmoid(z))
                mix[r0:r0 + BLOCK, tile_idx * LANES:(tile_idx + 1) * LANES] = gated.astype(BF16)

        def window_carry():
            kbuf[:, 0:BLOCK, :] = kbuf[:, TS:TS + BLOCK, :]
            vbuf[:, 0:BLOCK, :] = vbuf[:, TS:TS + BLOCK, :]

        def gla_gate():
            r_lo = proj[:, OFF_RG:OFF_RG + LANES].astype(BF16)
            gpre = jnp.dot(r_lo, wg_ref[...], preferred_element_type=F32) + bg_ref[...]
            gk_s[...] = ((jnp.minimum(gpre, 0.0) - jnp.log1p(jnp.exp(-jnp.abs(gpre))))
                         * (1.0 / GLA_GATE_NORMALIZER))

        def gla_decay(c):
            rows = slice(c * GLA_CHUNK, (c + 1) * GLA_CHUNK)
            g_c = gk_s[rows, :]
            g_hi = g_c.astype(BF16)
            g_lo = (g_c - g_hi.astype(F32)).astype(BF16)
            b_c = jnp.dot(tri2, jnp.concatenate([g_hi, g_lo], axis=0), preferred_element_type=F32)
            b_last = b_c[GLA_CHUNK - 1:GLA_CHUNK, :]
            qd[rows, :] = proj[rows, OFF_QG:OFF_QG + 256] * qscale * jnp.exp(b_c)
            kg = proj[rows, OFF_KG:OFF_KG + 256]
            kd[rows, :] = kg * jnp.exp(-b_c)
            ke[rows, :] = kg * jnp.exp(b_last - b_c)
            vals['decay', c] = jnp.exp(b_last)

        def gla_intra(c):
            rows = slice(c * GLA_CHUNK, (c + 1) * GLA_CHUNK)
            for p in range(GLA_HEADS // 2):
                cols = slice(p * LANES, (p + 1) * LANES)
                qt = qd[rows, cols]
                kst = _split_halves(kd[rows, cols], lo_c).astype(BF16)
                a = lax.dot_general(qt.astype(BF16), kst, _NT, preferred_element_type=F32)
                a = jnp.where(causal, a, 0.0)
                vals['ast', c, p] = _split_halves(a, lo_c).astype(BF16)

        def gla_out(c):
            rows = slice(c * GLA_CHUNK, (c + 1) * GLA_CHUNK)
            gg = gg_ref[...]
            for p in range(GLA_HEADS // 2):
                cols = slice(p * LANES, (p + 1) * LANES)
                vst = jnp.concatenate(
                    [proj[rows, OFF_VG + (2 * p) * LANES:OFF_VG + (2 * p + 1) * LANES],
                     proj[rows, OFF_VG + (2 * p + 1) * LANES:OFF_VG + (2 * p + 2) * LANES]], axis=0)
                qst = _split_halves(qd[rows, cols], lo_c).astype(BF16)
                st = state[p]
                o = (jnp.dot(vals.pop(('ast', c, p)), vst.astype(BF16), preferred_element_type=F32)
                     + lax.dot_general(qst, st.astype(BF16), _NT, preferred_element_type=F32))
                kest = _split_halves(ke[rows, cols], lo_c).astype(BF16)
                ut = jnp.dot(vst.T.astype(BF16), kest, preferred_element_type=F32)
                state[p] = st * vals['decay', c][:, cols] + ut

                oms = jnp.mean(o * o, axis=-1, keepdims=True)
                on = o * lax.rsqrt(oms + EPS) * gg
                for e in range(2):
                    head = 2 * p + e
                    z = proj[rows, OFF_ZG + head * LANES:OFF_ZG + (head + 1) * LANES]
                    gated = on[e * GLA_CHUNK:(e + 1) * GLA_CHUNK] * (z * _sigmoid(z))
                    mix[rows, OFF_MIX_GLA + head * LANES:OFF_MIX_GLA + (head + 1) * LANES] = (
                        gated.astype(BF16))

        P = functools.partial
        main = [gla_gate, window_build, P(swa_qk, 0), P(swa_qk, 1)]
        main += [P(gla_decay, c) for c in range(n_chunks)]
        for u in range(n_units):
            main.append(P(swa_pv, u))
            if u + 2 < n_units:
                main.append(P(swa_qk, u + 2))
        main += [window_carry, P(gla_intra, 0), P(gla_intra, 1)]
        for c in range(n_chunks):
            main.append(P(gla_out, c))
            if c + 2 < n_chunks:
                main.append(P(gla_intra, c + 2))
        fill = [P(out_chunk, mix2.at[nxt], r0, c0)
                for r0 in range(0, TS, SUB) for c0 in range(0, D_MODEL, PROJ_COLS)]
        fill += [P(norm_in, xn_ref), finalize]
        fill += [P(in_chunk, proj2.at[nxt], r0, c0)
                 for r0 in range(0, TS, SUB) for c0 in range(0, N_PAD, PROJ_COLS)]
        for piece in _interleave(main, fill):
            piece()

    @pl.when(first_call)
    def _prologue():
        qb = lax.broadcasted_iota(jnp.int32, (BLOCK, 2 * BLOCK), 0)
        cb = lax.broadcasted_iota(jnp.int32, (BLOCK, 2 * BLOCK), 1)
        n = jnp.maximum(qb + BLOCK - cb, 0)
        max_exact = NUM_BUCKETS // 2
        nf = jnp.maximum(n, 1).astype(F32)
        large = max_exact + jnp.floor(jnp.log(nf / max_exact) / math.log(MAX_DISTANCE / max_exact)
                                      * (NUM_BUCKETS - max_exact)).astype(jnp.int32)
        large = jnp.minimum(large, NUM_BUCKETS - 1)
        bucket = jnp.where(n < max_exact, n, large)

        def head_body(h, carry):
            def bucket_body(bk, acc):
                return jnp.where(bucket == bk, rb_ref[bk, h], acc)
            bias[h] = lax.fori_loop(0, NUM_BUCKETS, bucket_body,
                                    jnp.zeros((BLOCK, 2 * BLOCK), F32))
            return carry
        lax.fori_loop(0, ATTN_HEADS, head_body, 0)
        norm_in(xp_ref)
        for r0 in range(0, TS, SUB):
            for c0 in range(0, N_PAD, PROJ_COLS):
                in_chunk(proj2.at[0], r0, c0)
        mix2[1] = jnp.zeros((TS, MIX_WIDTH), BF16)

    @pl.when(seq_start)
    def _reset():
        kbuf[:, 0:BLOCK, :] = jnp.zeros((4, BLOCK, LANES), BF16)
        vbuf[:, 0:BLOCK, :] = jnp.zeros((4, BLOCK, LANES), BF16)
        state[...] = jnp.zeros(state.shape, F32)

    parity = lax.rem(g, 2)

    @pl.when(parity == 0)
    def _even():
        step(0, 1)

    @pl.when(parity == 1)
    def _odd():
        step(1, 0)


@jax.jit
def kernel(x, norm_gain, w_in, w_gate_up, b_gate, rel_bias, sinks, gla_norm_gain, w_out,
           final_norm_gain):
    batch, seq, d_model = x.shape
    assert d_model == D_MODEL and seq % TS == 0
    assert norm_gain.shape[0] == 1, "single-layer block"
    tiles_per_seq = seq // TS
    n_tiles = batch * tiles_per_seq
    w_in_p = jnp.pad(w_in[0], ((0, 0), (0, N_PAD - N_IN))).astype(BF16)
    wg_p = jnp.pad(w_gate_up[0], ((0, LANES - GLA_GATE_RANK), (0, 0))).astype(BF16)
    x2 = x.reshape(batch * seq, d_model)
    const = lambda *shape: pl.BlockSpec(shape, lambda g: (0,) * len(shape),
                                        pipeline_mode=pl.Buffered(1))
    smem = pl.BlockSpec(memory_space=pltpu.SMEM)
    prev_tile = lambda g: (jnp.maximum(g - 1, 0), 0)
    next_tile = lambda g: (jnp.minimum(g + 1, n_tiles - 1), 0)
    out = pl.pallas_call(
        functools.partial(_fused_kernel, tiles_per_seq),
        grid=(n_tiles + 1,),
        in_specs=[
            pl.BlockSpec((TS, D_MODEL), prev_tile),
            pl.BlockSpec((TS, D_MODEL), next_tile),
            const(1, D_MODEL),
            const(D_MODEL, N_PAD),
            const(LANES, 256),
            const(1, 256),
            smem,
            smem,
            const(1, GLA_VALUE_DIM),
            const(MIX_WIDTH, D_MODEL),
            const(1, D_MODEL),
        ],
        out_specs=pl.BlockSpec((TS, D_MODEL), prev_tile),
        out_shape=jax.ShapeDtypeStruct(x2.shape, x.dtype),
        scratch_shapes=[
            pltpu.VMEM((2, TS, N_PAD), F32),
            pltpu.VMEM((2, TS, MIX_WIDTH), BF16),
            pltpu.VMEM((TS, D_MODEL), BF16),
            pltpu.VMEM((TS, D_MODEL), F32),
            pltpu.VMEM((4, TS + BLOCK, LANES), BF16),
            pltpu.VMEM((4, TS + BLOCK, LANES), BF16),
            pltpu.VMEM((ATTN_HEADS, BLOCK, 2 * BLOCK), F32),
            pltpu.VMEM((GLA_HEADS // 2, GLA_VALUE_DIM, LANES), F32),
            pltpu.VMEM((TS, 256), F32),
            pltpu.VMEM((TS, 256), F32),
            pltpu.VMEM((TS, 256), F32),
            pltpu.VMEM((TS, 256), F32),
        ],
        compiler_params=pltpu.CompilerParams(
            dimension_semantics=("arbitrary",),
            vmem_limit_bytes=VMEM_LIMIT_BYTES),
        name="hybrid_swa_gla_block",
    )(x2, x2, norm_gain, w_in_p, wg_p, b_gate, rel_bias, sinks[0], gla_norm_gain,
      w_out[0].astype(BF16), final_norm_gain.reshape(1, D_MODEL))
    return out.reshape(batch, seq, d_model)
```
